```python
import jax, jax.numpy as jnp
from jax import lax
import numpy as np

D_MODEL = 1024
BATCH = 4
SEQ = 4096
DEPTH = 1

MIX_WIDTH = D_MODEL
ATT_HEADS = 8
ATT_HEAD_DIM = 64
IDX_HEADS = 8
IDX_DIM = 64
TOPK_MAX = 256
RET_HEADS = 4
RET_QK_DIM = 128
RET_V_DIM = 128
ROPE_THETA = 10000.0
Q_BLOCK = 128
RET_CHUNK = 128
PEER_HEADS = 8
PEER_NKEYS = 128
PEER_KEY_DIM = 256
PEER_HALF = PEER_KEY_DIM // 2
PEER_TOPK = 16
N_EXPERTS = PEER_NKEYS * PEER_NKEYS
N_MOD = 6
EPS = 1e-6

IN_SPLITS = (ATT_HEADS * ATT_HEAD_DIM, ATT_HEADS * ATT_HEAD_DIM, ATT_HEADS * ATT_HEAD_DIM,
             IDX_HEADS * IDX_DIM, IDX_DIM, IDX_HEADS,
             RET_HEADS * RET_QK_DIM, RET_HEADS * RET_QK_DIM, RET_HEADS * RET_V_DIM, RET_HEADS * RET_V_DIM)
IN_COLS = (3 * ATT_HEADS * ATT_HEAD_DIM + IDX_HEADS * IDX_DIM + IDX_DIM + IDX_HEADS
           + 2 * RET_HEADS * RET_QK_DIM + 2 * RET_HEADS * RET_V_DIM)

kernel_name = "hymba_dsa_retnet_peer_adaln"


def rmsnorm(x, g):
    xf = x.astype(jnp.float32)
    y = xf * lax.rsqrt(jnp.mean(xf * xf, axis=-1, keepdims=True) + EPS)
    return (y * g.astype(jnp.float32)).astype(x.dtype)


def rope_inv_freq(d):
    return 1.0 / (ROPE_THETA ** (jnp.arange(0, d, 2, dtype=jnp.float32) / d))


def retnet_inv_freq(d):
    return 1.0 / (ROPE_THETA ** jnp.linspace(0.0, 1.0, d // 2, dtype=jnp.float32))


def rotate(x, inv_freq):
    s = x.shape[1]
    ang = jnp.arange(s, dtype=jnp.float32)[:, None] * inv_freq[None, :]
    cos = jnp.cos(ang)[None, :, None, :]
    sin = jnp.sin(ang)[None, :, None, :]
    x1, x2 = jnp.split(x.astype(jnp.float32), 2, axis=-1)
    return jnp.concatenate([x1 * cos - x2 * sin, x1 * sin + x2 * cos], axis=-1).astype(x.dtype)


def dsa_attention(q, k, v, qi, ki, wi):
    b, s, h, dh = q.shape
    topk = min(TOPK_MAX, s // 4)
    n_blocks = s // Q_BLOCK
    scale = dh ** -0.5
    idx_scale = IDX_DIM ** -0.5
    spos = jnp.arange(s)

    def block(i):
        t0 = i * Q_BLOCK
        qb = lax.dynamic_slice_in_dim(q, t0, Q_BLOCK, axis=1)
        qib = lax.dynamic_slice_in_dim(qi, t0, Q_BLOCK, axis=1)
        wib = lax.dynamic_slice_in_dim(wi, t0, Q_BLOCK, axis=1)
        tpos = t0 + jnp.arange(Q_BLOCK)
        causal = spos[None, :] <= tpos[:, None]
        sc = jnp.einsum('bqhd,bsd->bqhs', qib, ki, preferred_element_type=jnp.float32) * idx_scale
        iscore = jnp.einsum('bqh,bqhs->bqs', wib.astype(jnp.float32), jax.nn.relu(sc))
        iscore = jnp.where(causal[None], iscore, -jnp.inf)
        _, sel = lax.top_k(iscore, topk)
        valid = sel <= tpos[None, :, None]
        kg = jax.vmap(lambda kb, ib: kb[ib])(k, sel)
        vg = jax.vmap(lambda vb, ib: vb[ib])(v, sel)
        logits = jnp.einsum('bqhd,bqkhd->bhqk', qb, kg, preferred_element_type=jnp.float32) * scale
        logits = jnp.where(valid[:, None], logits, -jnp.inf)
        p = jax.nn.softmax(logits, axis=-1)
        return jnp.einsum('bhqk,bqkhd->bqhd', p.astype(v.dtype), vg)

    out = lax.map(block, jnp.arange(n_blocks))
    return out.transpose(1, 0, 2, 3, 4).reshape(b, s, h * dh)


def retention_group(qr, kr, vr, gr):
    b, s, _ = qr.shape
    hh, dk, dv, cl = RET_HEADS, RET_QK_DIM, RET_V_DIM, RET_CHUNK
    n = s // cl
    f32 = jnp.float32
    inv = retnet_inv_freq(dk)
    q = rotate(qr.reshape(b, s, hh, dk), inv).astype(f32)
    k = rotate(kr.reshape(b, s, hh, dk), inv).astype(f32) * (dk ** -0.5)
    v = vr.reshape(b, s, hh, dv).astype(f32)

    def to_chunks(t):
        return t.reshape(b, n, cl, hh, t.shape[-1]).transpose(1, 0, 3, 2, 4)

    qc, kc, vc = to_chunks(q), to_chunks(k), to_chunks(v)
    log_g = jnp.log(1.0 - 2.0 ** (-5.0 - jnp.arange(hh, dtype=f32)))
    idx = jnp.arange(cl)
    diff = (idx[:, None] - idx[None, :]).astype(f32)
    dmat = jnp.where(diff >= 0, jnp.exp(log_g[:, None, None] * jnp.maximum(diff, 0.0)), 0.0)
    xi = jnp.exp(log_g[:, None] * (idx + 1).astype(f32))[None, :, :, None]
    zeta = jnp.exp(log_g[:, None] * (cl - 1 - idx).astype(f32))[None, :, :, None]
    g_chunk = jnp.exp(log_g * cl)[None, :, None, None]

    def step(state, inp):
        qi, ki, vi = inp
        inner = jnp.einsum('bhid,bhjd->bhij', qi, ki) * dmat[None]
        o = (jnp.einsum('bhij,bhjv->bhiv', inner, vi)
             + jnp.einsum('bhid,bhdv->bhiv', qi, state) * xi)
        state = g_chunk * state + jnp.einsum('bhjd,bhjv->bhdv', ki * zeta, vi)
        return state, o

    state0 = jnp.zeros((b, hh, dk, dv), f32)
    _, o = lax.scan(step, state0, (qc, kc, vc))
    o = o.transpose(1, 0, 3, 2, 4).reshape(b, s, hh, dv)
    mu = jnp.mean(o, axis=-1, keepdims=True)
    var = jnp.mean(jnp.square(o - mu), axis=-1, keepdims=True)
    o = ((o - mu) * lax.rsqrt(var + EPS)).reshape(b, s, hh * dv)
    return (jax.nn.silu(gr.astype(f32)) * o).astype(gr.dtype)


def peer_ffn(h, w_pq, sub_keys, u_emb, v_emb):
    b, s, d = h.shape
    nb = s // Q_BLOCK
    t = b * Q_BLOCK
    kk = PEER_TOPK
    hb = h.reshape(b, nb, Q_BLOCK, d).transpose(1, 0, 2, 3).reshape(nb, t, d)

    def block(xb):
        qry = (xb @ w_pq).reshape(t, PEER_HEADS, 2, PEER_HALF)
        sc = jnp.einsum('thpd,hpnd->thpn', qry, sub_keys, preferred_element_type=jnp.float32)
        v1, i1 = lax.top_k(sc[:, :, 0], kk)
        v2, i2 = lax.top_k(sc[:, :, 1], kk)
        cand = (v1[..., :, None] + v2[..., None, :]).reshape(t, PEER_HEADS, kk * kk)
        cid = (i1[..., :, None] * PEER_NKEYS + i2[..., None, :]).reshape(t, PEER_HEADS, kk * kk)
        top_s, pos = lax.top_k(cand, kk)
        eid = jnp.take_along_axis(cid, pos, axis=-1)
        gate = jax.nn.softmax(top_s, axis=-1)
        u = u_emb[eid]
        act = jax.nn.gelu(jnp.einsum('thkd,td->thk', u, xb, preferred_element_type=jnp.float32),
                          approximate=False)
        vv = v_emb[eid]
        return jnp.einsum('thk,thkd->td', (gate * act).astype(xb.dtype), vv)

    out = lax.map(block, hb)
    return out.reshape(nb, b, Q_BLOCK, d).transpose(1, 0, 2, 3).reshape(b, s, d)


def setup_inputs(seed: int = 0) -> dict:
    key = jax.random.key(seed)
    ks = jax.random.split(key, 13)
    f = jnp.float32
    d = D_MODEL
    nrm = jax.random.normal
    return {
        "x": nrm(ks[0], (BATCH, SEQ, d), f),
        "c": nrm(ks[1], (BATCH, d), f),
        "w_ada": nrm(ks[2], (DEPTH, d, N_MOD * d), f) * (0.5 * d ** -0.5),
        "b_ada": 0.01 * nrm(ks[3], (DEPTH, N_MOD * d), f),
        "norm_mix_g": 1.0 + 0.01 * nrm(ks[4], (DEPTH, d), f),
        "w_in": nrm(ks[5], (DEPTH, d, IN_COLS), f) * d ** -0.5,
        "w_out": nrm(ks[6], (DEPTH, MIX_WIDTH, d), f) * MIX_WIDTH ** -0.5,
        "norm_ffn_g": 1.0 + 0.01 * nrm(ks[7], (DEPTH, d), f),
        "w_pq": nrm(ks[8], (DEPTH, d, PEER_HEADS * PEER_KEY_DIM), f) * d ** -0.5,
        "sub_keys": nrm(ks[9], (DEPTH, PEER_HEADS, 2, PEER_NKEYS, PEER_HALF), f) * PEER_HALF ** -0.5,
        "u_emb": nrm(ks[10], (DEPTH, N_EXPERTS, d), f) * d ** -0.5,
        "v_emb": nrm(ks[11], (DEPTH, N_EXPERTS, d), f) * PEER_HEADS ** -0.5,
        "norm_final_g": 1.0 + 0.01 * nrm(ks[12], (d,), f),
    }


def reference(x, c, w_ada, b_ada, norm_mix_g, w_in, w_out, norm_ffn_g, w_pq, sub_keys, u_emb, v_emb,
              norm_final_g):
    b, s, _ = x.shape
    split_pts = tuple(int(p) for p in np.cumsum(np.array(IN_SPLITS))[:-1])
    rope_att = rope_inv_freq(ATT_HEAD_DIM)
    rope_idx = rope_inv_freq(IDX_DIM)
    for layer in range(DEPTH):
        mod = jax.nn.silu(c) @ w_ada[layer] + b_ada[layer]
        sh_m, sc_m, gt_m, sh_f, sc_f, gt_f = jnp.split(mod[:, None, :], N_MOD, axis=-1)

        h = rmsnorm(x, norm_mix_g[layer]) * (1 + sc_m) + sh_m
        proj = h @ w_in[layer]
        qa, ka, va, qi, ki, wi, qr, kr, vr, gr = jnp.split(proj, split_pts, axis=-1)
        qa = rotate(qa.reshape(b, s, ATT_HEADS, ATT_HEAD_DIM), rope_att)
        ka = rotate(ka.reshape(b, s, ATT_HEADS, ATT_HEAD_DIM), rope_att)
        va = va.reshape(b, s, ATT_HEADS, ATT_HEAD_DIM)
        qi = rotate(qi.reshape(b, s, IDX_HEADS, IDX_DIM), rope_idx)
        ki = rotate(ki.reshape(b, s, 1, IDX_DIM), rope_idx)[:, :, 0]
        wi = wi * (IDX_HEADS ** -0.5)
        att = dsa_attention(qa, ka, va, qi, ki, wi)
        ret = retention_group(qr, kr, vr, gr)
        mixed = jnp.concatenate([att, ret.astype(att.dtype)], axis=-1) @ w_out[layer]
        x = x + gt_m * mixed

        h = rmsnorm(x, norm_ffn_g[layer]) * (1 + sc_f) + sh_f
        x = x + gt_f * peer_ffn(h, w_pq[layer], sub_keys[layer], u_emb[layer], v_emb[layer])
    return rmsnorm(x, norm_final_g)
```

```python
import functools

import numpy as np
import jax
import jax.numpy as jnp
from jax import lax
from jax.experimental import pallas as pl
from jax.experimental.pallas import tpu as pltpu
from jax.experimental.pallas import tpu_sc as plsc

F32 = jnp.float32
BF16 = jnp.bfloat16
I32 = jnp.int32

D_MODEL = 1024
ATT_HEADS = 8
ATT_HEAD_DIM = 64
IDX_HEADS = 8
IDX_DIM = 64
TOPK_MAX = 256
RET_HEADS = 4
RET_QK_DIM = 128
RET_V_DIM = 128
RET_CHUNK = 128
ROPE_THETA = 10000.0
PEER_HEADS = 8
PEER_NKEYS = 128
PEER_HALF = 128
PEER_TOPK = 16
N_MOD = 6
EPS = 1e-6

LANES = 128
ATT_W = ATT_HEADS * ATT_HEAD_DIM
RET_W = RET_HEADS * RET_QK_DIM
HALF_D = D_MODEL // 2
PAIRS = PEER_HEADS * PEER_TOPK

TM_PROJ = 512
TQ_ATT = 256
TM_PEER = 256
TT_MIX = 16
SC_WINDOW = 64
VMEM_LIMIT = 56 * 1024 * 1024

NEG_BIG = -1e30
INT_MIN = np.int32(-2 ** 31)
KEY_NEG_INF = np.int32(np.array(-np.inf, np.float32).view(np.int32) ^ np.int32(0x7FFFFFFF))


def _cparams(sem):
    return pltpu.CompilerParams(dimension_semantics=sem, vmem_limit_bytes=VMEM_LIMIT)


def _adaln_kernel(c_ref, w_ref, b_ref, o_ref):
    c = c_ref[...]
    a = c * (1.0 / (1.0 + jnp.exp(-c)))
    o_ref[...] = jnp.dot(a, w_ref[...], preferred_element_type=F32) + b_ref[...]


def _adaln(c, w, b):
    bsz = c.shape[0]
    rows = ((bsz + 7) // 8) * 8
    cp = jnp.zeros((rows, D_MODEL), F32).at[:bsz].set(c)
    ncol = w.shape[1]
    out = pl.pallas_call(
        _adaln_kernel,
        grid=(ncol // D_MODEL,),
        in_specs=[pl.BlockSpec((rows, D_MODEL), lambda j: (0, 0)),
                  pl.BlockSpec((D_MODEL, D_MODEL), lambda j: (0, j)),
                  pl.BlockSpec((1, D_MODEL), lambda j: (0, j))],
        out_specs=pl.BlockSpec((rows, D_MODEL), lambda j: (0, j)),
        out_shape=jax.ShapeDtypeStruct((rows, ncol), F32),
        compiler_params=_cparams(("parallel",)),
        name="adaln",
    )(cp, w, b.reshape(1, ncol))
    return out[:bsz]


_C_QA, _C_QA_R = 0, 512
_C_KA, _C_KA_R = 1024, 1536
_C_VA = 2048
_C_QI, _C_QI_R = 2560, 3072
_C_KI, _C_KI_R = 3584, 3712
_C_WI = 3840
_C_QR, _C_QR_R = 3968, 4480
_C_KR, _C_KR_R = 4992, 5504
_C_VR = 6016
_C_GR = 6528
_C_END = 7040


def _rot_cols(w, heads, dh):
    d = w.shape[0]
    w4 = w.reshape(d, heads, 2, dh // 2)
    return jnp.concatenate([-w4[:, :, 1], w4[:, :, 0]], axis=-1).reshape(d, heads * dh)


def _prep_w_in(w_in):
    o = np.cumsum([0, ATT_W, ATT_W, ATT_W, IDX_HEADS * IDX_DIM, IDX_DIM, IDX_HEADS,
                   RET_W, RET_W, RET_W, RET_W])
    seg = [w_in[:, o[i]:o[i + 1]] for i in range(10)]
    qa, ka, va, qi, ki, wi, qr, kr, vr, gr = seg
    ki2 = jnp.concatenate([ki, ki], axis=1)
    ki2_r = jnp.concatenate([_rot_cols(ki, 1, IDX_DIM)] * 2, axis=1)
    wi_pad = jnp.concatenate([wi, jnp.zeros((w_in.shape[0], LANES - IDX_HEADS), w_in.dtype)], axis=1)
    cols = [qa, _rot_cols(qa, ATT_HEADS, ATT_HEAD_DIM),
            ka, _rot_cols(ka, ATT_HEADS, ATT_HEAD_DIM),
            va,
            qi, _rot_cols(qi, IDX_HEADS, IDX_DIM),
            ki2, ki2_r, wi_pad,
            qr, _rot_cols(qr, RET_HEADS, RET_QK_DIM),
            kr, _rot_cols(kr, RET_HEADS, RET_QK_DIM),
            vr, gr]
    return jnp.concatenate(cols, axis=1).astype(BF16)


def _rope_tables(s, inv_freq, heads):
    ang = jnp.arange(s, dtype=F32)[:, None] * inv_freq[None, :]
    cos = jnp.cos(ang)
    sin = jnp.sin(ang)
    return (jnp.tile(jnp.concatenate([cos, cos], axis=1), (1, heads)),
            jnp.tile(jnp.concatenate([sin, sin], axis=1), (1, heads)))


def _norm_mod(x, g, sc, sh):
    y = x * lax.rsqrt(jnp.mean(x * x, axis=-1, keepdims=True) + EPS)
    return (y * g) * (1.0 + sc) + sh


def _inproj_kernel(x_ref, sc_ref, sh_ref, g_ref, w_ref, cosa_ref, sina_ref, cosr_ref, sinr_ref,
                   q_ref, k_ref, v_ref, qi_ref, ki_ref, wi_ref, qr_ref, kr_ref, vr_ref, gr_ref):
    h = _norm_mod(x_ref[0], g_ref[...], sc_ref[0], sh_ref[0]).astype(BF16)

    def mm(c0, n):
        return jnp.dot(h, w_ref[:, c0:c0 + n], preferred_element_type=F32)

    cosa, sina = cosa_ref[...], sina_ref[...]
    cosr, sinr = cosr_ref[...], sinr_ref[...]
    att_scale = ATT_HEAD_DIM ** -0.5
    idx_scale = IDX_DIM ** -0.5
    q_ref[0] = ((mm(_C_QA, ATT_W) * cosa + mm(_C_QA_R, ATT_W) * sina) * att_scale).astype(BF16)
    k_ref[0] = (mm(_C_KA, ATT_W) * cosa + mm(_C_KA_R, ATT_W) * sina).astype(BF16)
    v_ref[0] = mm(_C_VA, ATT_W).astype(BF16)
    qi_ref[0] = ((mm(_C_QI, ATT_W) * cosa + mm(_C_QI_R, ATT_W) * sina) * idx_scale).astype(BF16)
    ki_ref[0] = (mm(_C_KI, LANES) * cosa[:, :LANES] + mm(_C_KI_R, LANES) * sina[:, :LANES]).astype(BF16)
    wi_ref[0] = mm(_C_WI, LANES) * (IDX_HEADS ** -0.5)
    qr_ref[0] = (mm(_C_QR, RET_W) * cosr + mm(_C_QR_R, RET_W) * sinr).astype(BF16)
    kr_ref[0] = ((mm(_C_KR, RET_W) * cosr + mm(_C_KR_R, RET_W) * sinr) * (RET_QK_DIM ** -0.5)).astype(BF16)
    vr_ref[0] = mm(_C_VR, RET_W).astype(BF16)
    gr_ref[0] = mm(_C_GR, RET_W)


def _inproj(x, sc, sh, g, w_all, cosa, sina, cosr, sinr):
    b, s, d = x.shape
    tm = min(TM_PROJ, s)
    tok = lambda w: pl.BlockSpec((1, tm, w), lambda bi, i: (bi, i, 0))
    tab = pl.BlockSpec((tm, ATT_W), lambda bi, i: (i, 0))
    modspec = pl.BlockSpec((1, 1, d), lambda bi, i: (bi, 0, 0))
    outs = [(ATT_W, BF16)] * 4 + [(LANES, BF16), (LANES, F32)] + [(RET_W, BF16)] * 3 + [(RET_W, F32)]
    return pl.pallas_call(
        _inproj_kernel,
        grid=(b, s // tm),
        in_specs=[tok(d), modspec, modspec,
                  pl.BlockSpec((1, d), lambda bi, i: (0, 0)),
                  pl.BlockSpec((d, _C_END), lambda bi, i: (0, 0)),
                  tab, tab, tab, tab],
        out_specs=[tok(w) for w, _ in outs],
        out_shape=[jax.ShapeDtypeStruct((b, s, w), dt) for w, dt in outs],
        compiler_params=_cparams(("parallel", "parallel")),
        name="inproj",
    )(x, sc, sh, g.reshape(1, d), w_all, cosa, sina, cosr, sinr)


def _dsa_kernel(topk, q_ref, qi_ref, wi_ref, k_ref, v_ref, ki_ref, o_ref,
                keys_ref, m_ref, l_ref, acc_ref):
    i = pl.program_id(1)
    tq = q_ref.shape[1]
    s_len = k_ref.shape[1]
    nch = i + 1
    row = i * tq + lax.broadcasted_iota(I32, (tq, tq), 0)
    col0 = lax.broadcasted_iota(I32, (tq, tq), 1)
    lane = lax.broadcasted_iota(I32, (tq, LANES), 1)
    lo_half = lane < ATT_HEAD_DIM
    nt = (((1,), (1,)), ((), ()))

    def head_q(ref, h):
        pair = ref[0, :, (h // 2) * LANES:(h // 2 + 1) * LANES]
        keep = lo_half if h % 2 == 0 else jnp.logical_not(lo_half)
        return jnp.where(keep, pair, jnp.zeros_like(pair))

    wi = wi_ref[0]

    def score_chunk(c, carry):
        kc = ki_ref[0, pl.ds(c * tq, tq), :]
        acc = jnp.zeros((tq, tq), F32)
        for h in range(IDX_HEADS):
            sc = lax.dot_general(head_q(qi_ref, h), kc, nt, preferred_element_type=F32)
            acc = acc + wi[:, h:h + 1] * jnp.maximum(sc, 0.0)
        acc = jnp.where(c * tq + col0 <= row, acc, -jnp.inf)
        bits = lax.bitcast_convert_type(acc, I32)
        bits = jnp.where(bits == INT_MIN, 0, bits)
        keys_ref[c] = jnp.where(bits < 0, bits ^ np.int32(0x7FFFFFFF), bits)
        return carry

    lax.fori_loop(0, nch, score_chunk, 0)

    def count(pred_fn):
        def body(c, acc):
            p = jnp.where(pred_fn(keys_ref[c], c * tq + col0), 1.0, 0.0)
            for j in range(tq // LANES):
                acc = acc + p[:, j * LANES:(j + 1) * LANES]
            return acc
        acc = lax.fori_loop(0, nch, body, jnp.zeros((tq, LANES), F32))
        return jnp.sum(acc, axis=1, keepdims=True)

    kf = float(topk)

    def bit_step(bi, u):
        cand_u = u | lax.shift_left(np.int32(1), 31 - bi)
        cand = cand_u ^ INT_MIN
        cnt = count(lambda key, col: key >= cand)
        return jnp.where(cnt >= kf, cand_u, u)

    u = lax.fori_loop(0, 32, bit_step, jnp.zeros((tq, 1), I32))
    thr = u ^ INT_MIN

    cnt_gt = count(lambda key, col: key > thr)
    cnt_eq = count(lambda key, col: key == thr)
    need = kf - cnt_gt
    tie_row = jnp.logical_and(cnt_eq > need, thr != KEY_NEG_INF)
    any_tie = jnp.max(jnp.where(tie_row, 1.0, 0.0)) > 0.0
    nbits = int(np.log2(s_len))

    def tie_search():
        def step(bi, lim):
            cand = lim | lax.shift_left(np.int32(1), nbits - 1 - bi)
            cnt = count(lambda key, col: jnp.logical_and(key == thr, col < cand))
            return jnp.where(cnt < need, cand, lim)
        lim = lax.fori_loop(0, nbits, step, jnp.zeros((tq, 1), I32))
        return jnp.where(tie_row, lim, s_len)

    last_eq = lax.cond(any_tie, tie_search, lambda: jnp.full((tq, 1), s_len, I32))

    m_ref[...] = jnp.full(m_ref.shape, NEG_BIG, F32)
    l_ref[...] = jnp.zeros(l_ref.shape, F32)
    acc_ref[...] = jnp.zeros(acc_ref.shape, F32)

    def attend(c, carry):
        key = keys_ref[c]
        col = c * tq + col0
        sel = jnp.logical_or(key > thr, jnp.logical_and(key == thr, col <= last_eq))
        sel = jnp.logical_and(sel, col <= row)
        kc = k_ref[0, pl.ds(c * tq, tq), :]
        vc = v_ref[0, pl.ds(c * tq, tq), :]
        for h in range(ATT_HEADS):
            ps = slice((h // 2) * LANES, (h // 2 + 1) * LANES)
            sc = lax.dot_general(head_q(q_ref, h), kc[:, ps], nt, preferred_element_type=F32)
            sc = jnp.where(sel, sc, NEG_BIG)
            m_old = m_ref[h]
            m_new = jnp.maximum(m_old, jnp.max(sc, axis=1, keepdims=True))
            p = jnp.exp(sc - jnp.concatenate([m_new] * (tq // LANES), axis=1))
            p = jnp.where(sel, p, 0.0)
            alpha = jnp.exp(m_old - m_new)
            l_ref[h] = alpha * l_ref[h] + jnp.sum(p, axis=1, keepdims=True)
            acc_ref[h] = alpha * acc_ref[h] + jnp.dot(p.astype(BF16), vc[:, ps],
                                                     preferred_element_type=F32)
            m_ref[h] = m_new
        return carry

    lax.fori_loop(0, nch, attend, 0)

    for pr in range(ATT_HEADS // 2):
        even = acc_ref[2 * pr] / l_ref[2 * pr]
        odd = acc_ref[2 * pr + 1] / l_ref[2 * pr + 1]
        o_ref[0, :, pr * LANES:(pr + 1) * LANES] = jnp.where(lo_half, even, odd).astype(o_ref.dtype)


def _dsa(q, k, v, qi, ki, wi):
    b, s, _ = q.shape
    tq = min(TQ_ATT, s)
    nq = s // tq
    topk = min(TOPK_MAX, s // 4)
    qspec = lambda w: pl.BlockSpec((1, tq, w), lambda bi, i: (bi, i, 0))
    full = lambda w: pl.BlockSpec((1, s, w), lambda bi, i: (bi, 0, 0))
    return pl.pallas_call(
        functools.partial(_dsa_kernel, topk),
        grid=(b, nq),
        in_specs=[qspec(ATT_W), qspec(ATT_W), qspec(LANES), full(ATT_W), full(ATT_W), full(LANES)],
        out_specs=qspec(ATT_W),
        out_shape=jax.ShapeDtypeStruct((b, s, ATT_W), BF16),
        scratch_shapes=[pltpu.VMEM((nq, tq, tq), I32),
                        pltpu.VMEM((ATT_HEADS, tq, LANES), F32),
                        pltpu.VMEM((ATT_HEADS, tq, LANES), F32),
                        pltpu.VMEM((ATT_HEADS, tq, LANES), F32)],
        compiler_params=_cparams(("parallel", "arbitrary")),
        name="dsa",
    )(q, qi, wi, k, v, ki)


def _ret_kernel(q_ref, k_ref, v_ref, g_ref, dmat_ref, xi_ref, zeta_ref, gch_ref, o_ref):
    cl = RET_CHUNK
    n = q_ref.shape[1] // cl
    dmat = dmat_ref[0]
    xi = xi_ref[0]
    zeta = zeta_ref[0]
    gch = gch_ref[0]
    nt = (((1,), (1,)), ((), ()))

    def step(j, state):
        rows = pl.ds(j * cl, cl)
        qc = q_ref[0, rows, :]
        kc = k_ref[0, rows, :]
        vc = v_ref[0, rows, :]
        inner = lax.dot_general(qc, kc, nt, preferred_element_type=F32) * dmat
        o = (jnp.dot(inner.astype(BF16), vc, preferred_element_type=F32)
             + jnp.dot(qc, state.astype(BF16), preferred_element_type=F32) * xi)
        kz = (kc.astype(F32) * zeta).T.astype(BF16)
        state = gch * state + jnp.dot(kz, vc, preferred_element_type=F32)
        mu = jnp.mean(o, axis=-1, keepdims=True)
        var = jnp.mean(jnp.square(o - mu), axis=-1, keepdims=True)
        on = (o - mu) * lax.rsqrt(var + EPS)
        gate = g_ref[0, rows, :]
        o_ref[0, rows, :] = ((gate * (1.0 / (1.0 + jnp.exp(-gate)))) * on).astype(o_ref.dtype)
        return state

    lax.fori_loop(0, n, step, jnp.zeros((RET_QK_DIM, RET_V_DIM), F32))


def _ret_consts():
    hh, cl = RET_HEADS, RET_CHUNK
    log_g = jnp.log(1.0 - 2.0 ** (-5.0 - jnp.arange(hh, dtype=F32)))
    idx = jnp.arange(cl)
    diff = (idx[:, None] - idx[None, :]).astype(F32)
    dmat = jnp.where(diff >= 0, jnp.exp(log_g[:, None, None] * jnp.maximum(diff, 0.0)), 0.0)
    xi = jnp.exp(log_g[:, None] * (idx + 1).astype(F32))
    zeta = jnp.exp(log_g[:, None] * (cl - 1 - idx).astype(F32))
    gch = jnp.exp(log_g * cl)
    wide = lambda t: jnp.broadcast_to(t[:, :, None], (hh, cl, LANES))
    return dmat, wide(xi), wide(zeta), jnp.broadcast_to(gch[:, None, None], (hh, cl, LANES))


def _retention(qr, kr, vr, gr):
    b, s, _ = qr.shape
    dmat, xi, zeta, gch = _ret_consts()
    seq = pl.BlockSpec((1, s, LANES), lambda bi, h: (bi, 0, h))
    cst = pl.BlockSpec((1, RET_CHUNK, LANES), lambda bi, h: (h, 0, 0))
    return pl.pallas_call(
        _ret_kernel,
        grid=(b, RET_HEADS),
        in_specs=[seq, seq, seq, seq, cst, cst, cst, cst],
        out_specs=seq,
        out_shape=jax.ShapeDtypeStruct((b, s, RET_W), BF16),
        compiler_params=_cparams(("parallel", "parallel")),
        name="retention",
    )(qr, kr, vr, gr, dmat, xi, zeta, gch)


def _topk_rows(sc, k, payload=None):
    n = sc.shape[0]
    iota = lax.broadcasted_iota(I32, sc.shape, 0)
    vals, idxs = [], []
    for _ in range(k):
        m = jnp.max(sc, axis=0, keepdims=True)
        idx = jnp.min(jnp.where(sc == m, iota, n), axis=0, keepdims=True)
        hit = iota == idx
        vals.append(m)
        if payload is None:
            idxs.append(idx)
        else:
            idxs.append(jnp.sum(jnp.where(hit, payload, 0), axis=0, keepdims=True))
        sc = jnp.where(hit, -jnp.inf, sc)
    return jnp.concatenate(vals, axis=0), jnp.concatenate(idxs, axis=0)


def _peer_sel_kernel(att_ref, ret_ref, x_ref, gt_ref, sc_ref, sh_ref, g_ref,
                     wout_ref, wpq_ref, keys_ref, x1_ref, h2_ref, eid_ref, gate_ref):
    mixed = (jnp.dot(att_ref[0], wout_ref[:ATT_W, :], preferred_element_type=F32)
             + jnp.dot(ret_ref[0], wout_ref[ATT_W:, :], preferred_element_type=F32))
    x1 = x_ref[0] + gt_ref[0] * mixed
    x1_ref[0] = x1
    h2 = _norm_mod(x1, g_ref[...], sc_ref[0], sh_ref[0]).astype(BF16)
    h2_ref[0] = h2
    nt = (((1,), (1,)), ((), ()))
    kk = PEER_TOPK
    for h in range(PEER_HEADS):
        tops = []
        for p in range(2):
            c0 = (h * 2 + p) * PEER_HALF
            qry = jnp.dot(h2, wpq_ref[:, c0:c0 + PEER_HALF], preferred_element_type=F32).astype(BF16)
            sc_t = lax.dot_general(keys_ref[h * 2 + p], qry, nt, preferred_element_type=F32)
            tops.append(_topk_rows(sc_t, kk))
        (v1, i1), (v2, i2) = tops
        cand = jnp.concatenate([v1[a:a + 1] + v2 for a in range(kk)], axis=0)
        cid = jnp.concatenate([i1[a:a + 1] * PEER_NKEYS + i2 for a in range(kk)], axis=0)
        top_s, eid = _topk_rows(cand, kk, payload=cid)
        e = jnp.exp(top_s - top_s[0:1])
        gate = e / jnp.sum(e, axis=0, keepdims=True)
        eid_ref[0, 0, h * kk:(h + 1) * kk, :] = eid
        gate_ref[0, 0, h * kk:(h + 1) * kk, :] = gate


def _peer_select(att, ret, x, gt, sc, sh, g, w_out, w_pq, sub_keys):
    b, s, d = x.shape
    tm = min(TM_PEER, s)
    nt_ = s // tm
    tok = lambda w: pl.BlockSpec((1, tm, w), lambda bi, i: (bi, i, 0))
    modspec = pl.BlockSpec((1, 1, d), lambda bi, i: (bi, 0, 0))
    tr = pl.BlockSpec((1, 1, PAIRS, tm), lambda bi, i: (bi, i, 0, 0))
    nk = 2 * PEER_HEADS
    return pl.pallas_call(
        _peer_sel_kernel,
        grid=(b, nt_),
        in_specs=[tok(ATT_W), tok(RET_W), tok(d), modspec, modspec, modspec,
                  pl.BlockSpec((1, d), lambda bi, i: (0, 0)),
                  pl.BlockSpec((d, d), lambda bi, i: (0, 0)),
                  pl.BlockSpec((d, nk * PEER_HALF), lambda bi, i: (0, 0)),
                  pl.BlockSpec((nk, PEER_NKEYS, PEER_HALF), lambda bi, i: (0, 0, 0))],
        out_specs=[tok(d), tok(d), tr, tr],
        out_shape=[jax.ShapeDtypeStruct((b, s, d), F32),
                   jax.ShapeDtypeStruct((b, s, d), BF16),
                   jax.ShapeDtypeStruct((b, nt_, PAIRS, tm), I32),
                   jax.ShapeDtypeStruct((b, nt_, PAIRS, tm), F32)],
        compiler_params=_cparams(("parallel", "parallel")),
        name="peer_select",
    )(att, ret, x, gt, sc, sh, g.reshape(1, d), w_out, w_pq, sub_keys)


def _pack_table(t):
    tb = lax.bitcast_convert_type(t.astype(BF16), jnp.uint16).astype(jnp.uint32)
    words = tb[:, :HALF_D] | (tb[:, HALF_D:] << 16)
    return lax.bitcast_convert_type(words, I32)


def _gather_rows(table, idx):
    n = idx.shape[0]
    width = table.shape[1]
    info = plsc.get_sparse_core_info()
    workers = info.num_cores * info.num_subcores
    per_w = n // workers
    assert per_w * workers == n and per_w % SC_WINDOW == 0
    mesh = plsc.VectorSubcoreMesh(core_axis_name="c", subcore_axis_name="s")

    @pl.kernel(out_type=jax.ShapeDtypeStruct((n, width), table.dtype), mesh=mesh,
               scratch_types=[pltpu.VMEM((per_w,), I32),
                              pltpu.VMEM((SC_WINDOW, width), table.dtype)],
               name="peer_gather")
    def gather(tab_hbm, idx_hbm, out_hbm, idx_v, rows_v):
        wid = lax.axis_index("s") * info.num_cores + lax.axis_index("c")
        base = wid * per_w
        pltpu.sync_copy(idx_hbm.at[pl.ds(base, per_w)], idx_v)

        @pl.loop(0, per_w // SC_WINDOW)
        def _(j):
            off = j * SC_WINDOW
            pltpu.sync_copy(tab_hbm.at[idx_v.at[pl.ds(off, SC_WINDOW)]], rows_v)
            pltpu.sync_copy(rows_v, out_hbm.at[pl.ds(base + off, SC_WINDOW)])

    return gather(table, idx)


def _gelu(x):
    return 0.5 * x * (1.0 + lax.erf(x * np.float32(1.0 / np.sqrt(2.0))))


def _peer_mix_kernel(final_norm, ug_ref, vg_ref, h_ref, gate_ref, x1_ref, gt_ref, gf_ref, o_ref):
    tt = h_ref.shape[0]
    nt = (((1,), (1,)), ((), ()))
    h = h_ref[...]
    hs = jnp.concatenate([h[:, :HALF_D], h[:, HALF_D:]], axis=0)
    rowz = lax.broadcasted_iota(I32, (tt, 2 * PAIRS), 0)
    odd = (lax.broadcasted_iota(I32, (tt, 2 * PAIRS), 1) & 1) == 1

    z = jnp.zeros((tt, 2 * PAIRS), F32)
    for t in range(tt):
        a = pltpu.bitcast(ug_ref[t * PAIRS:(t + 1) * PAIRS, :], BF16)
        pr = lax.dot_general(hs, a, nt, preferred_element_type=F32)
        z = jnp.where(rowz == t, jnp.where(odd, pr[tt:], pr[:tt]), z)
    act = jnp.where(odd, z + pltpu.roll(z, 1, 1), z + pltpu.roll(z, 2 * PAIRS - 1, 1))
    w = gate_ref[...] * _gelu(act)
    ws = jnp.concatenate([jnp.where(odd, 0.0, w), jnp.where(odd, w, 0.0)], axis=0).astype(BF16)

    rowo = lax.broadcasted_iota(I32, (tt, HALF_D), 0)
    out_lo = jnp.zeros((tt, HALF_D), F32)
    out_hi = jnp.zeros((tt, HALF_D), F32)
    for t in range(tt):
        bmat = pltpu.bitcast(vg_ref[t * PAIRS:(t + 1) * PAIRS, :], BF16)
        po = jnp.dot(ws, bmat, preferred_element_type=F32)
        out_lo = jnp.where(rowo == t, po[:tt], out_lo)
        out_hi = jnp.where(rowo == t, po[tt:], out_hi)
    peer = jnp.concatenate([out_lo, out_hi], axis=1)
    x2 = x1_ref[...] + gt_ref[0] * peer
    if final_norm:
        x2 = x2 * lax.rsqrt(jnp.mean(x2 * x2, axis=-1, keepdims=True) + EPS) * gf_ref[...]
    o_ref[...] = x2


def _peer_mix(ug, vg, h2, gate2, x1, gt, gf, final_norm):
    s, d = x1.shape
    tt = TT_MIX
    rows = pl.BlockSpec((tt * PAIRS, HALF_D), lambda i: (i, 0))
    return pl.pallas_call(
        functools.partial(_peer_mix_kernel, final_norm),
        grid=(s // tt,),
        in_specs=[rows, rows,
                  pl.BlockSpec((tt, d), lambda i: (i, 0)),
                  pl.BlockSpec((tt, 2 * PAIRS), lambda i: (i, 0)),
                  pl.BlockSpec((tt, d), lambda i: (i, 0)),
                  pl.BlockSpec((1, 1, d), lambda i: (0, 0, 0)),
                  pl.BlockSpec((1, d), lambda i: (0, 0))],
        out_specs=pl.BlockSpec((tt, d), lambda i: (i, 0)),
        out_shape=jax.ShapeDtypeStruct((s, d), F32),
        compiler_params=_cparams(("parallel",)),
        name="peer_mix",
    )(ug, vg, h2, gate2, x1, gt, gf.reshape(1, d))


def kernel(x, c, w_ada, b_ada, norm_mix_g, w_in, w_out, norm_ffn_g, w_pq, sub_keys, u_emb, v_emb,
           norm_final_g):
    b, s, d = x.shape
    depth = w_ada.shape[0]
    inv_att = 1.0 / (ROPE_THETA ** (jnp.arange(0, ATT_HEAD_DIM, 2, dtype=F32) / ATT_HEAD_DIM))
    inv_ret = 1.0 / (ROPE_THETA ** jnp.linspace(0.0, 1.0, RET_QK_DIM // 2, dtype=F32))
    cosa, sina = _rope_tables(s, inv_att, ATT_HEADS)
    cosr, sinr = _rope_tables(s, inv_ret, RET_HEADS)
    for layer in range(depth):
        mod = _adaln(c, w_ada[layer], b_ada[layer])
        sh_m, sc_m, gt_m, sh_f, sc_f, gt_f = [m.reshape(b, 1, d) for m in jnp.split(mod, N_MOD, axis=-1)]

        q, k, v, qi, ki, wi, qr, kr, vr, gr = _inproj(
            x, sc_m, sh_m, norm_mix_g[layer], _prep_w_in(w_in[layer]), cosa, sina, cosr, sinr)
        att = _dsa(q, k, v, qi, ki, wi)
        ret = _retention(qr, kr, vr, gr)

        keys = sub_keys[layer].reshape(2 * PEER_HEADS, PEER_NKEYS, PEER_HALF).astype(BF16)
        x1, h2, eid_t, gate_t = _peer_select(
            att, ret, x, gt_m, sc_f, sh_f, norm_ffn_g[layer],
            w_out[layer].astype(BF16), w_pq[layer].astype(BF16), keys)
        eid = eid_t.transpose(0, 1, 3, 2).reshape(b, s * PAIRS)
        gate2 = jnp.repeat(gate_t.transpose(0, 1, 3, 2).reshape(b, s, PAIRS), 2, axis=-1)

        u_tab = _pack_table(u_emb[layer])
        v_tab = _pack_table(v_emb[layer])
        outs = []
        for bi in range(b):
            ug = _gather_rows(u_tab, eid[bi])
            vg = _gather_rows(v_tab, eid[bi])
            outs.append(_peer_mix(ug, vg, h2[bi], gate2[bi], x1[bi], gt_f[bi:bi + 1], norm_final_g,
                                  layer == depth - 1))
        x = jnp.stack(outs, axis=0)
    return x
```
